```python
import math
import jax, jax.numpy as jnp
from jax import lax
import numpy as np

D_MODEL = 1024
BATCH = 1
SEQ = 16384
DEPTH = 4
DEC_BATCH = 16
DEC_SEQ = 64
PAST_LEN = 4096

CHUNK = 64
Q_BLOCK = 128
EPS = 1e-6
NEG_INF = -1e30

H_A = 8
Q_LORA = 256
KV_LORA = 128
NOPE_DIM = 64
ROPE_DIM = 32
V_DIM_A = 64
ROPE_THETA = 10000.0
MLA_SCALE = 1.0 / math.sqrt(NOPE_DIM + ROPE_DIM)

H_B = 8
HD_B = 64
FOX_SCALE = 1.0 / math.sqrt(HD_B)
FORGET_BIAS_INIT = 2.0

H_C = 8
HD_C = 64
DIFF_SCALE = 1.0 / math.sqrt(HD_C)

NUM_BUCKETS = 32
MAX_DISTANCE = 128

D_FF = 2816
CONV_W = 3

N_AB = DEPTH // 2
N_C = DEPTH - N_AB
SIZES_AB = (Q_LORA, KV_LORA, ROPE_DIM, H_B * HD_B, H_B * HD_B, H_B * HD_B, H_B)
IN_AB = Q_LORA + KV_LORA + ROPE_DIM + 3 * H_B * HD_B + H_B
OUT_AB = H_A * V_DIM_A + H_B * HD_B
OUT_C = H_C * 2 * HD_C
IN_C = 3 * OUT_C

kernel_name = "hybrid_streaming_encoder_step"


def rmsnorm(x, g):
    x32 = x.astype(jnp.float32)
    y = x32 * lax.rsqrt(jnp.mean(x32 * x32, axis=-1, keepdims=True) + EPS)
    return (y * g.astype(jnp.float32)).astype(x.dtype)


def split_cols(z, sizes):
    out, start = [], 0
    for n in sizes:
        out.append(z[..., start:start + n])
        start += n
    return out


def rope(x, pos):
    half = ROPE_DIM // 2
    inv = ROPE_THETA ** (-jnp.arange(half, dtype=jnp.float32) / half)
    ang = pos.astype(jnp.float32)[:, None] * inv[None, :]
    cos = jnp.cos(ang)[None, :, None, :]
    sin = jnp.sin(ang)[None, :, None, :]
    x32 = x.astype(jnp.float32)
    x1, x2 = x32[..., :half], x32[..., half:]
    return jnp.concatenate([x1 * cos - x2 * sin, x2 * cos + x1 * sin], axis=-1).astype(x.dtype)


def t5_bucket(rel):
    nb = NUM_BUCKETS // 2
    max_exact = nb // 2
    ret = jnp.where(rel > 0, nb, 0)
    n = jnp.abs(rel)
    large = max_exact + (jnp.log(jnp.maximum(n, 1).astype(jnp.float32) / max_exact)
                         / math.log(MAX_DISTANCE / max_exact) * (nb - max_exact)).astype(jnp.int32)
    large = jnp.minimum(large, nb - 1)
    return ret + jnp.where(n < max_exact, n, large)


def masked_softmax(s, mask):
    return jax.nn.softmax(jnp.where(mask, s, NEG_INF), axis=-1)


def sweep_query_blocks(fn, q_arrays, qpos):
    B, T = q_arrays[0].shape[:2]
    qb = min(Q_BLOCK, T)
    nb = T // qb
    xs = tuple(jnp.swapaxes(a.reshape((B, nb, qb) + a.shape[2:]), 0, 1) for a in q_arrays)
    xs = xs + (qpos.reshape(nb, qb),)
    out = lax.map(lambda blk: fn(*blk), xs)
    return jnp.swapaxes(out, 0, 1).reshape((B, T) + out.shape[3:])


def mixer_ab(h, caches, w_in, b_f, g_q, g_kv, w_q_up, w_kv_up, w_out):
    ckv_c, kpe_c, fk_c, fv_c, flf_c = caches
    B, T, _ = h.shape
    P = ckv_c.shape[1]
    qpos = P + jnp.arange(T, dtype=jnp.int32)
    kpos = jnp.arange(P + T, dtype=jnp.int32)
    kchunk = kpos // CHUNK
    q_lat, kv_lat, k_rope, qf, kf, vf, f_raw = split_cols(h @ w_in, SIZES_AB)

    c_q = rmsnorm(q_lat, g_q)
    q = (c_q @ w_q_up).reshape(B, T, H_A, NOPE_DIM + ROPE_DIM)
    q_nope = q[..., :NOPE_DIM]
    q_pe = rope(q[..., NOPE_DIM:], qpos)
    c_kv = rmsnorm(kv_lat, g_kv)
    k_pe = rope(k_rope[:, :, None, :], qpos)[:, :, 0, :]
    ckv_all = jnp.concatenate([ckv_c, c_kv], axis=1)
    kpe_all = jnp.concatenate([kpe_c, k_pe], axis=1)
    kv = (ckv_all @ w_kv_up).reshape(B, P + T, H_A, NOPE_DIM + V_DIM_A)
    k_nope, v_a = kv[..., :NOPE_DIM], kv[..., NOPE_DIM:]

    def mla_block(qn, qp, qpos_b):
        s = (jnp.einsum('bqhd,bkhd->bhqk', qn, k_nope, preferred_element_type=jnp.float32)
             + jnp.einsum('bqhr,bkr->bhqk', qp, kpe_all, preferred_element_type=jnp.float32)) * MLA_SCALE
        p = masked_softmax(s, kchunk[None, :] <= (qpos_b // CHUNK)[:, None])
        return jnp.einsum('bhqk,bkhd->bqhd', p.astype(v_a.dtype), v_a)

    o_a = sweep_query_blocks(mla_block, (q_nope, q_pe), qpos)

    qf = qf.reshape(B, T, H_B, HD_B)
    kf = kf.reshape(B, T, H_B, HD_B)
    vf = vf.reshape(B, T, H_B, HD_B)
    logf = jax.nn.log_sigmoid((f_raw + b_f).astype(jnp.float32))
    fk_all = jnp.concatenate([fk_c, kf], axis=1)
    fv_all = jnp.concatenate([fv_c, vf], axis=1)
    cum = jnp.cumsum(jnp.concatenate([flf_c.astype(jnp.float32), logf], axis=1), axis=1)
    cum_k = jnp.swapaxes(cum, 1, 2)
    cum_q = cum[:, P:]

    def fox_block(qb_, cq, qpos_b):
        s = (jnp.einsum('bqhd,bkhd->bhqk', qb_, fk_all, preferred_element_type=jnp.float32) * FOX_SCALE
             + jnp.swapaxes(cq, 1, 2)[..., None] - cum_k[:, :, None, :])
        p = masked_softmax(s, kpos[None, :] <= qpos_b[:, None])
        return jnp.einsum('bhqk,bkhd->bqhd', p.astype(fv_all.dtype), fv_all)

    o_b = sweep_query_blocks(fox_block, (qf, cum_q), qpos)

    o = jnp.concatenate([o_a.reshape(B, T, H_A * V_DIM_A), o_b.reshape(B, T, H_B * HD_B)], axis=-1)
    return o @ w_out, (c_kv, k_pe, kf, vf, logf)


def mixer_c(h, caches, w_in, lq1, lk1, lq2, lk2, g_sub, w_out, rel_bias, lambda_init):
    k_c, v_c = caches
    B, T, _ = h.shape
    P = k_c.shape[1]
    qpos = P + jnp.arange(T, dtype=jnp.int32)
    kpos = jnp.arange(P + T, dtype=jnp.int32)
    kchunk = kpos // CHUNK
    q, k, v = split_cols(h @ w_in, (OUT_C, OUT_C, OUT_C))
    q = q.reshape(B, T, H_C, 2 * HD_C)
    k = k.reshape(B, T, H_C, 2 * HD_C)
    v = v.reshape(B, T, H_C, 2 * HD_C)
    k_all = jnp.concatenate([k_c, k], axis=1)
    v_all = jnp.concatenate([v_c, v], axis=1)
    k1, k2 = k_all[..., :HD_C], k_all[..., HD_C:]
    f32 = jnp.float32
    lam = (jnp.exp(jnp.sum(lq1.astype(f32) * lk1.astype(f32)))
           - jnp.exp(jnp.sum(lq2.astype(f32) * lk2.astype(f32))) + lambda_init)

    def diff_block(qb_, qpos_b):
        q1, q2 = qb_[..., :HD_C], qb_[..., HD_C:]
        bias = rel_bias[t5_bucket(kpos[None, :] - qpos_b[:, None])]
        bias = jnp.transpose(bias, (2, 0, 1)).astype(f32)[None]
        mask = kchunk[None, :] <= (qpos_b // CHUNK)[:, None]
        s1 = jnp.einsum('bqhd,bkhd->bhqk', q1, k1, preferred_element_type=f32) * DIFF_SCALE + bias
        s2 = jnp.einsum('bqhd,bkhd->bhqk', q2, k2, preferred_element_type=f32) * DIFF_SCALE + bias
        p = masked_softmax(s1, mask) - lam * masked_softmax(s2, mask)
        return jnp.einsum('bhqk,bkhd->bqhd', p.astype(v_all.dtype), v_all)

    o = sweep_query_blocks(diff_block, (q,), qpos)
    o = rmsnorm(o, g_sub) * (1.0 - lambda_init)
    return o.reshape(B, T, OUT_C) @ w_out, (k, v)


def conv_ffn(h, conv_prev, w_in, conv_w, conv_b, w_down):
    T = h.shape[1]
    gate, up = split_cols(h @ w_in, (D_FF, D_FF))
    gp = jnp.concatenate([conv_prev.astype(gate.dtype), gate], axis=1)
    conv = conv_b + gp[:, 0:T] * conv_w[0]
    for j in range(1, CONV_W):
        conv = conv + gp[:, j:j + T] * conv_w[j]
    y = (jax.nn.gelu(conv, approximate=False) * up) @ w_down
    return y, gp[:, T:]


def diff_lambda_init(layer_idx):
    return 0.8 - 0.6 * math.exp(-0.3 * layer_idx)


def empty_caches(layer_idx, batch, dtype):
    if layer_idx % 2 == 0:
        shapes = [(batch, 0, KV_LORA), (batch, 0, ROPE_DIM), (batch, 0, H_B, HD_B),
                  (batch, 0, H_B, HD_B), (batch, 0, H_B)]
    else:
        shapes = [(batch, 0, H_C, 2 * HD_C), (batch, 0, H_C, 2 * HD_C)]
    return tuple(jnp.zeros(s, dtype) for s in shapes)


def setup_inputs(seed: int = 0) -> dict:
    key = jax.random.key(seed)
    ks = iter(jax.random.split(key, 64))

    def nrm(shape, scale=1.0):
        return jax.random.normal(next(ks), shape, jnp.float32) * scale

    inp = {}
    inp["x_prompt"] = nrm((BATCH, SEQ, D_MODEL))
    inp["x_sample"] = nrm((DEC_BATCH, DEC_SEQ, D_MODEL))
    for l in range(DEPTH):
        if l % 2 == 0:
            inp[f"cache_mla_ckv_{l}"] = nrm((DEC_BATCH, PAST_LEN, KV_LORA))
            inp[f"cache_mla_kpe_{l}"] = nrm((DEC_BATCH, PAST_LEN, ROPE_DIM))
            inp[f"cache_fox_k_{l}"] = nrm((DEC_BATCH, PAST_LEN, H_B, HD_B))
            inp[f"cache_fox_v_{l}"] = nrm((DEC_BATCH, PAST_LEN, H_B, HD_B))
            inp[f"cache_fox_logf_{l}"] = jax.nn.log_sigmoid(nrm((DEC_BATCH, PAST_LEN, H_B)) + FORGET_BIAS_INIT)
        else:
            inp[f"cache_diff_k_{l}"] = nrm((DEC_BATCH, PAST_LEN, H_C, 2 * HD_C))
            inp[f"cache_diff_v_{l}"] = nrm((DEC_BATCH, PAST_LEN, H_C, 2 * HD_C))
    inp["state_ffn_conv"] = nrm((DEPTH, DEC_BATCH, CONV_W - 1, D_FF))

    inp["g_norm_mix"] = 1.0 + nrm((DEPTH, D_MODEL), 0.02)
    inp["g_norm_ffn"] = 1.0 + nrm((DEPTH, D_MODEL), 0.02)
    inp["g_norm_final"] = 1.0 + nrm((D_MODEL,), 0.02)

    inp["w_in_ab"] = nrm((N_AB, D_MODEL, IN_AB), D_MODEL ** -0.5)
    inp["b_fgate"] = FORGET_BIAS_INIT + nrm((N_AB, H_B), 0.1)
    inp["g_q_lora"] = 1.0 + nrm((N_AB, Q_LORA), 0.02)
    inp["g_kv_lora"] = 1.0 + nrm((N_AB, KV_LORA), 0.02)
    inp["w_q_up"] = nrm((N_AB, Q_LORA, H_A * (NOPE_DIM + ROPE_DIM)), Q_LORA ** -0.5)
    inp["w_kv_up"] = nrm((N_AB, KV_LORA, H_A * (NOPE_DIM + V_DIM_A)), KV_LORA ** -0.5)
    inp["w_out_ab"] = nrm((N_AB, OUT_AB, D_MODEL), OUT_AB ** -0.5)

    inp["w_in_c"] = nrm((N_C, D_MODEL, IN_C), D_MODEL ** -0.5)
    inp["lambda_q1"] = nrm((N_C, HD_C), 0.1)
    inp["lambda_k1"] = nrm((N_C, HD_C), 0.1)
    inp["lambda_q2"] = nrm((N_C, HD_C), 0.1)
    inp["lambda_k2"] = nrm((N_C, HD_C), 0.1)
    inp["g_subln"] = 1.0 + nrm((N_C, 2 * HD_C), 0.02)
    inp["w_out_c"] = nrm((N_C, OUT_C, D_MODEL), OUT_C ** -0.5)
    inp["rel_bias"] = nrm((NUM_BUCKETS, H_C), 0.2)

    inp["w_ffn_in"] = nrm((DEPTH, D_MODEL, 2 * D_FF), D_MODEL ** -0.5)
    inp["conv_w"] = nrm((DEPTH, CONV_W, D_FF), CONV_W ** -0.5)
    inp["conv_b"] = nrm((DEPTH, D_FF), 0.01)
    inp["w_ffn_down"] = nrm((DEPTH, D_FF, D_MODEL), D_FF ** -0.5)
    return inp


def reference(x_prompt, x_sample,
              cache_mla_ckv_0, cache_mla_kpe_0, cache_fox_k_0, cache_fox_v_0, cache_fox_logf_0,
              cache_diff_k_1, cache_diff_v_1,
              cache_mla_ckv_2, cache_mla_kpe_2, cache_fox_k_2, cache_fox_v_2, cache_fox_logf_2,
              cache_diff_k_3, cache_diff_v_3,
              state_ffn_conv,
              g_norm_mix, g_norm_ffn, g_norm_final,
              w_in_ab, b_fgate, g_q_lora, g_kv_lora, w_q_up, w_kv_up, w_out_ab,
              w_in_c, lambda_q1, lambda_k1, lambda_q2, lambda_k2, g_subln, w_out_c, rel_bias,
              w_ffn_in, conv_w, conv_b, w_ffn_down):
    def run_layer(x, caches, conv_prev, l):
        i = l // 2
        h = rmsnorm(x, g_norm_mix[l])
        if l % 2 == 0:
            y, rows = mixer_ab(h, caches, w_in_ab[i], b_fgate[i], g_q_lora[i], g_kv_lora[i],
                               w_q_up[i], w_kv_up[i], w_out_ab[i])
        else:
            y, rows = mixer_c(h, caches, w_in_c[i], lambda_q1[i], lambda_k1[i], lambda_q2[i],
                              lambda_k2[i], g_subln[i], w_out_c[i], rel_bias, diff_lambda_init(l))
        x = x + y
        f, conv_rows = conv_ffn(rmsnorm(x, g_norm_ffn[l]), conv_prev, w_ffn_in[l], conv_w[l],
                                conv_b[l], w_ffn_down[l])
        return x + f, rows, conv_rows

    sample_caches = [
        (cache_mla_ckv_0, cache_mla_kpe_0, cache_fox_k_0, cache_fox_v_0, cache_fox_logf_0),
        (cache_diff_k_1, cache_diff_v_1),
        (cache_mla_ckv_2, cache_mla_kpe_2, cache_fox_k_2, cache_fox_v_2, cache_fox_logf_2),
        (cache_diff_k_3, cache_diff_v_3),
    ]
    bp = x_prompt.shape[0]
    xp, xs = x_prompt, x_sample
    p_rows, s_rows, p_convs, s_convs = [], [], [], []
    for l in range(DEPTH):
        xp, rows_p, conv_p = run_layer(xp, empty_caches(l, bp, xp.dtype),
                                       jnp.zeros((bp, CONV_W - 1, D_FF), xp.dtype), l)
        xs, rows_s, conv_s = run_layer(xs, sample_caches[l], state_ffn_conv[l], l)
        p_rows.append(rows_p)
        s_rows.append(rows_s)
        p_convs.append(conv_p)
        s_convs.append(conv_s)
    y_prompt = rmsnorm(xp, g_norm_final)
    y_sample = rmsnorm(xs, g_norm_final)
    (p_ckv0, p_kpe0, p_fk0, p_fv0, p_flf0), (p_dk1, p_dv1), \
        (p_ckv2, p_kpe2, p_fk2, p_fv2, p_flf2), (p_dk3, p_dv3) = p_rows
    (s_ckv0, s_kpe0, s_fk0, s_fv0, s_flf0), (s_dk1, s_dv1), \
        (s_ckv2, s_kpe2, s_fk2, s_fv2, s_flf2), (s_dk3, s_dv3) = s_rows
    p_conv = jnp.stack(p_convs, axis=0)
    s_conv = jnp.stack(s_convs, axis=0)
    return (y_prompt, y_sample,
            p_ckv0, p_kpe0, p_fk0, p_fv0, p_flf0, p_dk1, p_dv1,
            p_ckv2, p_kpe2, p_fk2, p_fv2, p_flf2, p_dk3, p_dv3, p_conv,
            s_ckv0, s_kpe0, s_fk0, s_fv0, s_flf0, s_dk1, s_dv1,
            s_ckv2, s_kpe2, s_fk2, s_fv2, s_flf2, s_dk3, s_dv3, s_conv)
```

```python
import functools
import math

import jax
import jax.numpy as jnp
from jax import lax
from jax.experimental import pallas as pl
from jax.experimental.pallas import tpu as pltpu

F32 = jnp.float32
BF16 = jnp.bfloat16

D_MODEL = 1024
DEPTH = 4
CHUNK = 64
CHUNK_SHIFT = 6
EPS = 1e-6
NEG_INF = -1e30

H_A = 8
Q_LORA = 256
KV_LORA = 128
NOPE_DIM = 64
ROPE_DIM = 32
V_DIM_A = 64
ROPE_THETA = 10000.0
MLA_SCALE = 1.0 / math.sqrt(NOPE_DIM + ROPE_DIM)

H_B = 8
HD_B = 64
FOX_SCALE = 1.0 / math.sqrt(HD_B)

H_C = 8
HD_C = 64
DIFF_SCALE = 1.0 / math.sqrt(HD_C)

NUM_BUCKETS = 32
MAX_DISTANCE = 128
T5_FAR = 91

D_FF = 2816
CONV_W = 3

LANES = 128
SUBLANES = 8
VMEM_LIMIT_BYTES = 56 * 1024 * 1024

ROW_TILE = 512
ATTN_TILE = 512
FF_CHUNK = 256

AB_QLAT = (0, 256)
AB_KVLAT = (256, 384)
AB_KR = (384, 512)
AB_KRS = (512, 640)
AB_FR = (640, 768)
AB_QF = (768, 1280)
AB_KF = (1280, 1792)
AB_VF = (1792, 2304)
AB_COLS = 2304


def _cparams(*sem):
    return pltpu.CompilerParams(dimension_semantics=sem, vmem_limit_bytes=VMEM_LIMIT_BYTES)


def _rms(x, g):
    return x * lax.rsqrt(jnp.mean(x * x, axis=-1, keepdims=True) + EPS) * g


def _dot(a, b):
    return jnp.dot(a, b, preferred_element_type=F32)


def _dot_nt(a, b):
    return lax.dot_general(a, b, (((1,), (1,)), ((), ())), preferred_element_type=F32)


def _const_spec(shape):
    nd = len(shape)
    return pl.BlockSpec(shape, lambda *_: (0,) * nd, pipeline_mode=pl.Buffered(1))


def _proj_ab_kernel(x_ref, g_ref, w1_ref, gq_ref, gkv_ref, bf_ref, wq_ref, wkv_ref, cos_ref, sin_ref,
                    ckv_ref, kpe_ref, kf_ref, vf_ref, logf_ref,
                    qcat_ref, kcat_ref, qfe_ref, qfo_ref, kfb_ref, vfb_ref):
    x = x_ref[...]
    h = _rms(x, g_ref[...]).astype(BF16)

    def seg(ab):
        return _dot(h, w1_ref[:, ab[0]:ab[1]])

    cos = cos_ref[...]
    sin = sin_ref[...]

    c_q = _rms(seg(AB_QLAT), gq_ref[...]).astype(BF16)
    q_all = _dot(c_q, wq_ref[...])
    cos2 = jnp.concatenate([cos, cos], axis=1)
    sin2 = jnp.concatenate([sin, sin], axis=1)
    pe_rot = q_all[:, 512:768] * cos2 + q_all[:, 768:1024] * sin2
    lane = lax.broadcasted_iota(jnp.int32, (1, LANES), 1)
    for hh in range(H_A):
        nope_h = q_all[:, hh * NOPE_DIM:(hh + 1) * NOPE_DIM].astype(BF16)
        w_uk = wkv_ref[:, hh * 128:hh * 128 + NOPE_DIM]
        q_abs = _dot_nt(nope_h, w_uk)
        grp = hh // 4
        pe_g = pe_rot[:, grp * LANES:(grp + 1) * LANES]
        pe_m = jnp.where((lane // ROPE_DIM) == (hh % 4), pe_g, 0.0)
        qcat_ref[:, hh * 256:hh * 256 + 128] = (q_abs * MLA_SCALE).astype(BF16)
        qcat_ref[:, hh * 256 + 128:(hh + 1) * 256] = (pe_m * MLA_SCALE).astype(BF16)

    c_kv = _rms(seg(AB_KVLAT), gkv_ref[...])
    kpe4 = seg(AB_KR) * cos + seg(AB_KRS) * sin
    ckv_ref[...] = c_kv
    kpe_ref[...] = kpe4[:, :ROPE_DIM]
    kcat_ref[:, 0:128] = c_kv.astype(BF16)
    kcat_ref[:, 128:256] = kpe4.astype(BF16)

    qf = seg(AB_QF) * FOX_SCALE
    kf = seg(AB_KF)
    vf = seg(AB_VF)
    kf_ref[...] = kf
    vf_ref[...] = vf
    kfb_ref[...] = kf.astype(BF16)
    vfb_ref[...] = vf.astype(BF16)
    lane_b = lax.broadcasted_iota(jnp.int32, (1, H_B * HD_B), 1)
    even = ((lane_b // HD_B) % 2) == 0
    qfe_ref[...] = jnp.where(even, qf, 0.0).astype(BF16)
    qfo_ref[...] = jnp.where(even, 0.0, qf).astype(BF16)
    z = seg(AB_FR) + bf_ref[...]
    lf = jnp.minimum(z, 0.0) - jnp.log1p(jnp.exp(-jnp.abs(z)))
    logf_ref[...] = lf[:, :H_B]


def _proj_ab(x, g, w1, gq, gkv, bfp, wq, wkv, cos, sin):
    r = x.shape[0]
    tm = min(ROW_TILE, r)
    row = lambda n: pl.BlockSpec((tm, n), lambda i: (i, 0))
    out_shape = (
        jax.ShapeDtypeStruct((r, KV_LORA), F32), jax.ShapeDtypeStruct((r, ROPE_DIM), F32),
        jax.ShapeDtypeStruct((r, H_B * HD_B), F32), jax.ShapeDtypeStruct((r, H_B * HD_B), F32),
        jax.ShapeDtypeStruct((r, H_B), F32),
        jax.ShapeDtypeStruct((r, H_A * 256), BF16), jax.ShapeDtypeStruct((r, 256), BF16),
        jax.ShapeDtypeStruct((r, H_B * HD_B), BF16), jax.ShapeDtypeStruct((r, H_B * HD_B), BF16),
        jax.ShapeDtypeStruct((r, H_B * HD_B), BF16), jax.ShapeDtypeStruct((r, H_B * HD_B), BF16),
    )
    return pl.pallas_call(
        _proj_ab_kernel,
        grid=(r // tm,),
        in_specs=[row(D_MODEL), _const_spec((1, D_MODEL)), _const_spec((D_MODEL, AB_COLS)),
                  _const_spec((1, Q_LORA)), _const_spec((1, KV_LORA)), _const_spec((1, LANES)),
                  _const_spec((Q_LORA, 1024)), _const_spec((KV_LORA, 1024)), row(LANES), row(LANES)],
        out_specs=[row(KV_LORA), row(ROPE_DIM), row(512), row(512), row(H_B),
                   row(H_A * 256), row(256), row(512), row(512), row(512), row(512)],
        out_shape=out_shape,
        compiler_params=_cparams("arbitrary"),
        name="proj_ab",
    )(x, g, w1, gq, gkv, bfp, wq, wkv, cos, sin)


def _proj_c_kernel(x_ref, g_ref, w_ref, k_ref, v_ref, q1_ref, q2_ref, kb_ref, vb_ref):
    x = x_ref[...]
    h = _rms(x, g_ref[...]).astype(BF16)
    n = H_C * 2 * HD_C
    q = _dot(h, w_ref[:, 0:n]) * DIFF_SCALE
    k = _dot(h, w_ref[:, n:2 * n])
    v = _dot(h, w_ref[:, 2 * n:3 * n])
    k_ref[...] = k
    v_ref[...] = v
    kb_ref[...] = k.astype(BF16)
    vb_ref[...] = v.astype(BF16)
    lane = lax.broadcasted_iota(jnp.int32, (1, n), 1)
    first = ((lane // HD_C) % 2) == 0
    q1_ref[...] = jnp.where(first, q, 0.0).astype(BF16)
    q2_ref[...] = jnp.where(first, 0.0, q).astype(BF16)


def _proj_c(x, g, w):
    r = x.shape[0]
    tm = min(ROW_TILE, r)
    n = H_C * 2 * HD_C
    row = lambda m: pl.BlockSpec((tm, m), lambda i: (i, 0))
    out_shape = (jax.ShapeDtypeStruct((r, n), F32), jax.ShapeDtypeStruct((r, n), F32),
                 jax.ShapeDtypeStruct((r, n), BF16), jax.ShapeDtypeStruct((r, n), BF16),
                 jax.ShapeDtypeStruct((r, n), BF16), jax.ShapeDtypeStruct((r, n), BF16))
    return pl.pallas_call(
        _proj_c_kernel,
        grid=(r // tm,),
        in_specs=[row(D_MODEL), _const_spec((1, D_MODEL)), _const_spec((D_MODEL, 3 * n))],
        out_specs=[row(n)] * 6,
        out_shape=out_shape,
        compiler_params=_cparams("arbitrary"),
        name="proj_c",
    )(x, g, w)


def _split3(x):
    hi = x.astype(BF16)
    r1 = x - hi.astype(F32)
    mid = r1.astype(BF16)
    lo = (r1 - mid.astype(F32)).astype(BF16)
    return hi, mid, lo


def _cumsum_kernel(x_ref, o_ref, *, nc_shift):
    nc = x_ref.shape[2]
    n = H_B * nc
    x = x_ref[0].reshape(n, LANES)
    r = lax.broadcasted_iota(jnp.int32, (LANES, LANES), 0)
    c = lax.broadcasted_iota(jnp.int32, (LANES, LANES), 1)
    upper = (r <= c).astype(BF16)
    hi, mid, lo = _split3(x)
    within = _dot(hi, upper) + _dot(mid, upper) + _dot(lo, upper)
    rr = lax.broadcasted_iota(jnp.int32, (n, n), 0)
    cc = lax.broadcasted_iota(jnp.int32, (n, n), 1)
    before = (((rr >> nc_shift) == (cc >> nc_shift)) & (cc < rr)).astype(BF16)
    tot = jnp.broadcast_to(within[:, LANES - 1:LANES], (n, LANES))
    thi, tmid, tlo = _split3(tot)
    off = _dot(before, thi) + _dot(before, tmid) + _dot(before, tlo)
    o_ref[0] = (within + off).reshape(H_B, nc, LANES)


def _cumsum_time(lf_t):
    b, _, l = lf_t.shape
    nc = l // LANES
    nc_shift = nc.bit_length() - 1
    assert nc == 1 << nc_shift and nc % SUBLANES == 0
    x4 = lf_t.reshape(b, H_B, nc, LANES)
    spec = pl.BlockSpec((1, H_B, nc, LANES), lambda i: (i, 0, 0, 0))
    out = pl.pallas_call(
        functools.partial(_cumsum_kernel, nc_shift=nc_shift),
        grid=(b,),
        in_specs=[spec],
        out_specs=spec,
        out_shape=jax.ShapeDtypeStruct(x4.shape, F32),
        compiler_params=_cparams("arbitrary"),
        name="fox_cumsum",
    )(x4)
    return out.reshape(b, H_B, l)


def _online_softmax_step(s, v, m_ref, l_ref, acc_ref, idx):
    m_prev = m_ref[idx]
    m_new = jnp.maximum(m_prev, jnp.max(s, axis=1, keepdims=True))
    alpha = jnp.exp(m_prev - m_new)
    p = jnp.exp(s - m_new)
    l_ref[idx] = alpha * l_ref[idx] + jnp.sum(p, axis=1, keepdims=True)
    acc_ref[idx] = alpha * acc_ref[idx] + _dot(p.astype(BF16), v)
    m_ref[idx] = m_new


def _chunk_last(pos):
    return (pos // CHUNK) * CHUNK + CHUNK - 1


def _init_softmax_state(m_ref, l_ref, acc_ref):
    m_ref[...] = jnp.full(m_ref.shape, NEG_INF, F32)
    l_ref[...] = jnp.zeros(l_ref.shape, F32)
    acc_ref[...] = jnp.zeros(acc_ref.shape, F32)


def _flash_mla_kernel(q_ref, k_ref, wkv_ref, o_ref, m_ref, l_ref, acc_ref, *, tq, tk, past, nk):
    qi = pl.program_id(1)
    ki = pl.program_id(2)

    @pl.when(ki == 0)
    def _():
        _init_softmax_state(m_ref, l_ref, acc_ref)

    q_lo = past + qi * tq
    k_lo = ki * tk
    needed = k_lo <= _chunk_last(q_lo + tq - 1)
    full = (k_lo + tk - 1) <= _chunk_last(q_lo)

    def body(masked):
        k = k_ref[0]
        v = k[:, :KV_LORA]
        if masked:
            row = lax.broadcasted_iota(jnp.int32, (tq, tk), 0) + q_lo
            col = lax.broadcasted_iota(jnp.int32, (tq, tk), 1) + k_lo
            mask = (col >> CHUNK_SHIFT) <= (row >> CHUNK_SHIFT)
        for h in range(H_A):
            s = _dot_nt(q_ref[0, :, h * 256:(h + 1) * 256], k)
            if masked:
                s = jnp.where(mask, s, NEG_INF)
            _online_softmax_step(s, v, m_ref, l_ref, acc_ref, h)

    @pl.when(needed & full)
    def _():
        body(False)

    @pl.when(needed & jnp.logical_not(full))
    def _():
        body(True)

    @pl.when(ki == nk - 1)
    def _():
        for g in range(H_A // 2):
            outs = []
            for h in (2 * g, 2 * g + 1):
                o_lat = (acc_ref[h] / l_ref[h]).astype(BF16)
                w_uv = wkv_ref[:, h * 128 + NOPE_DIM:(h + 1) * 128]
                outs.append(_dot(o_lat, w_uv))
            o_ref[0, :, g * LANES:(g + 1) * LANES] = jnp.concatenate(outs, axis=1).astype(BF16)


def _last_needed_block_chunk(qi, tq, tk, past):
    return _chunk_last(past + qi * tq + tq - 1) // tk


def _flash_mla(qcat, kcat, wkv, past):
    b, t, _ = qcat.shape
    tkp = kcat.shape[1]
    tq = min(ATTN_TILE, t)
    tk = min(ATTN_TILE, tkp)
    nq, nk = t // tq, tkp // tk
    kmap = lambda bi, qi, ki: (bi, jnp.minimum(ki, _last_needed_block_chunk(qi, tq, tk, past)), 0)
    return pl.pallas_call(
        functools.partial(_flash_mla_kernel, tq=tq, tk=tk, past=past, nk=nk),
        grid=(b, nq, nk),
        in_specs=[pl.BlockSpec((1, tq, H_A * 256), lambda bi, qi, ki: (bi, qi, 0)),
                  pl.BlockSpec((1, tk, 256), kmap),
                  _const_spec((KV_LORA, 1024))],
        out_specs=pl.BlockSpec((1, tq, H_A * V_DIM_A), lambda bi, qi, ki: (bi, qi, 0)),
        out_shape=jax.ShapeDtypeStruct((b, t, H_A * V_DIM_A), BF16),
        scratch_shapes=[pltpu.VMEM((H_A, tq, 1), F32), pltpu.VMEM((H_A, tq, 1), F32),
                        pltpu.VMEM((H_A, tq, KV_LORA), F32)],
        compiler_params=_cparams("arbitrary", "arbitrary", "arbitrary"),
        name="flash_mla",
    )(qcat, kcat, wkv)


def _flash_fox_kernel(qe_ref, qo_ref, k_ref, v_ref, cq_ref, ck_ref, o_ref, m_ref, l_ref, acc_ref,
                      *, tq, tk, past, nk):
    qi = pl.program_id(1)
    ki = pl.program_id(2)

    @pl.when(ki == 0)
    def _():
        _init_softmax_state(m_ref, l_ref, acc_ref)

    q_lo = past + qi * tq
    k_lo = ki * tk
    needed = k_lo <= q_lo + tq - 1
    full = (k_lo + tk - 1) <= q_lo

    def body(masked):
        if masked:
            row = lax.broadcasted_iota(jnp.int32, (tq, tk), 0) + q_lo
            col = lax.broadcasted_iota(jnp.int32, (tq, tk), 1) + k_lo
            mask = col <= row
        cq = cq_ref[0]
        ck = ck_ref[0]
        for g in range(H_B // 2):
            kg = k_ref[0, :, g * LANES:(g + 1) * LANES]
            vg = v_ref[0, :, g * LANES:(g + 1) * LANES]
            for par, q_ref in ((0, qe_ref), (1, qo_ref)):
                h = 2 * g + par
                s = _dot_nt(q_ref[0, :, g * LANES:(g + 1) * LANES], kg)
                s = s + cq[:, h:h + 1] - ck[h:h + 1, :]
                if masked:
                    s = jnp.where(mask, s, NEG_INF)
                _online_softmax_step(s, vg, m_ref, l_ref, acc_ref, h)

    @pl.when(needed & full)
    def _():
        body(False)

    @pl.when(needed & jnp.logical_not(full))
    def _():
        body(True)

    @pl.when(ki == nk - 1)
    def _():
        lane = lax.broadcasted_iota(jnp.int32, (1, LANES), 1)
        for g in range(H_B // 2):
            oe = acc_ref[2 * g] / l_ref[2 * g]
            oo = acc_ref[2 * g + 1] / l_ref[2 * g + 1]
            o_ref[0, :, g * LANES:(g + 1) * LANES] = jnp.where(lane < HD_B, oe, oo).astype(BF16)


def _flash_fox(qe, qo, kb, vb, cq, ck, past):
    b, t, n = qe.shape
    tkp = kb.shape[1]
    tq = min(ATTN_TILE, t)
    tk = min(ATTN_TILE, tkp)
    nq, nk = t // tq, tkp // tk
    last = lambda qi: (past + qi * tq + tq - 1) // tk
    qspec = pl.BlockSpec((1, tq, n), lambda bi, qi, ki: (bi, qi, 0))
    kspec = pl.BlockSpec((1, tk, n), lambda bi, qi, ki: (bi, jnp.minimum(ki, last(qi)), 0))
    return pl.pallas_call(
        functools.partial(_flash_fox_kernel, tq=tq, tk=tk, past=past, nk=nk),
        grid=(b, nq, nk),
        in_specs=[qspec, qspec, kspec, kspec,
                  pl.BlockSpec((1, tq, H_B), lambda bi, qi, ki: (bi, qi, 0)),
                  pl.BlockSpec((1, H_B, tk), lambda bi, qi, ki: (bi, 0, jnp.minimum(ki, last(qi))))],
        out_specs=qspec,
        out_shape=jax.ShapeDtypeStruct((b, t, n), BF16),
        scratch_shapes=[pltpu.VMEM((H_B, tq, 1), F32), pltpu.VMEM((H_B, tq, 1), F32),
                        pltpu.VMEM((H_B, tq, LANES), F32)],
        compiler_params=_cparams("arbitrary", "arbitrary", "arbitrary"),
        name="flash_fox",
    )(qe, qo, kb, vb, cq, ck)


def _bias_tile_kernel(tab_ref, o_ref, *, tq, tk):
    d = (pl.program_id(0) - 1) * tk
    h = pl.program_id(1)
    row = lax.broadcasted_iota(jnp.int32, (tq, tk), 0)
    col = lax.broadcasted_iota(jnp.int32, (tq, tk), 1) + d
    rel = col - row
    nb = NUM_BUCKETS // 2
    max_exact = nb // 2
    ret = jnp.where(rel > 0, nb, 0)
    n = jnp.abs(rel)
    large = max_exact + (jnp.log(jnp.maximum(n, 1).astype(F32) / max_exact)
                         / math.log(MAX_DISTANCE / max_exact) * (nb - max_exact)).astype(jnp.int32)
    large = jnp.minimum(large, nb - 1)
    bucket = ret + jnp.where(n < max_exact, n, large)
    bias = jnp.zeros((tq, tk), F32)
    for bkt in range(NUM_BUCKETS):
        bias = jnp.where(bucket == bkt, tab_ref[bkt * H_C + h], bias)
    mask = (col >> CHUNK_SHIFT) <= (row >> CHUNK_SHIFT)
    o_ref[0, 0] = jnp.where(mask, bias, NEG_INF)


def _bias_tiles(rel_bias, tq, tk):
    return pl.pallas_call(
        functools.partial(_bias_tile_kernel, tq=tq, tk=tk),
        grid=(2, H_C),
        in_specs=[pl.BlockSpec(memory_space=pltpu.SMEM)],
        out_specs=pl.BlockSpec((1, 1, tq, tk), lambda di, h: (di, h, 0, 0)),
        out_shape=jax.ShapeDtypeStruct((2, H_C, tq, tk), F32),
        compiler_params=_cparams("arbitrary", "arbitrary"),
        name="t5_bias_tiles",
    )(rel_bias.reshape(NUM_BUCKETS * H_C))


def _flash_diff_kernel(far_ref, q1_ref, q2_ref, k_ref, v_ref, bt_ref, lq1_ref, lk1_ref, lq2_ref, lk2_ref,
                       gsub_ref, o_ref, m_ref, l_ref, acc_ref, *, tq, tk, past, nk, lambda_init):
    qi = pl.program_id(1)
    ki = pl.program_id(2)

    @pl.when(ki == 0)
    def _():
        _init_softmax_state(m_ref, l_ref, acc_ref)

    q_lo = past + qi * tq
    k_lo = ki * tk
    needed = k_lo <= _chunk_last(q_lo + tq - 1)
    far = (k_lo + tk - 1 - q_lo) <= -T5_FAR
    diag = k_lo == q_lo

    def body(tile_idx):
        for h in range(H_C):
            kh = k_ref[0, :, h * LANES:(h + 1) * LANES]
            vh = v_ref[0, :, h * LANES:(h + 1) * LANES]
            bias = far_ref[h] if tile_idx is None else bt_ref[tile_idx, h]
            for mp, q_ref in ((0, q1_ref), (1, q2_ref)):
                s = _dot_nt(q_ref[0, :, h * LANES:(h + 1) * LANES], kh) + bias
                _online_softmax_step(s, vh, m_ref, l_ref, acc_ref, mp * H_C + h)

    @pl.when(needed & far)
    def _():
        body(None)

    @pl.when(needed & jnp.logical_not(far) & jnp.logical_not(diag))
    def _():
        body(0)

    @pl.when(needed & jnp.logical_not(far) & diag)
    def _():
        body(1)

    @pl.when(ki == nk - 1)
    def _():
        lam = (jnp.exp(jnp.sum(lq1_ref[...] * lk1_ref[...], axis=1, keepdims=True))
               - jnp.exp(jnp.sum(lq2_ref[...] * lk2_ref[...], axis=1, keepdims=True)) + lambda_init)
        g = gsub_ref[...]
        for h in range(H_C):
            o = acc_ref[h] / l_ref[h] - lam * (acc_ref[H_C + h] / l_ref[H_C + h])
            o = _rms(o, g) * (1.0 - lambda_init)
            o_ref[0, :, h * LANES:(h + 1) * LANES] = o.astype(BF16)


def _flash_diff(far_bias, q1, q2, kb, vb, bias_tiles, lq1, lk1, lq2, lk2, gsub, past, lambda_init):
    b, t, n = q1.shape
    tkp = kb.shape[1]
    tq = min(ATTN_TILE, t)
    tk = min(ATTN_TILE, tkp)
    nq, nk = t // tq, tkp // tk
    assert past % tk == 0 and (tq == tk or nq == 1) and tk >= T5_FAR and tq <= tk
    assert bias_tiles.shape == (2, H_C, tq, tk)
    kmap = lambda bi, qi, ki: (bi, jnp.minimum(ki, _last_needed_block_chunk(qi, tq, tk, past)), 0)
    qspec = pl.BlockSpec((1, tq, n), lambda bi, qi, ki: (bi, qi, 0))
    kspec = pl.BlockSpec((1, tk, n), kmap)
    vec = _const_spec((1, HD_C))
    return pl.pallas_call(
        functools.partial(_flash_diff_kernel, tq=tq, tk=tk, past=past, nk=nk, lambda_init=lambda_init),
        grid=(b, nq, nk),
        in_specs=[pl.BlockSpec(memory_space=pltpu.SMEM), qspec, qspec, kspec, kspec,
                  _const_spec((2, H_C, tq, tk)), vec, vec, vec, vec, _const_spec((1, 2 * HD_C))],
        out_specs=qspec,
        out_shape=jax.ShapeDtypeStruct((b, t, n), BF16),
        scratch_shapes=[pltpu.VMEM((2 * H_C, tq, 1), F32), pltpu.VMEM((2 * H_C, tq, 1), F32),
                        pltpu.VMEM((2 * H_C, tq, 2 * HD_C), F32)],
        compiler_params=_cparams("arbitrary", "arbitrary", "arbitrary"),
        name="flash_diff",
    )(far_bias, q1, q2, kb, vb, bias_tiles, lq1, lk1, lq2, lk2, gsub)


def _ffn_kernel(*refs, n_o, splits, tm, seq_len, final_norm):
    x_ref = refs[0]
    o_refs = refs[1:1 + n_o]
    (wout_ref, gffn_ref, wg_ref, wu_ref, cw_ref, cb_ref, wd_ref, st_ref, gfin_ref,
     y_ref, conv_ref, gate_ref, carry_ref, act_ref) = refs[1 + n_o:]
    i = pl.program_id(0)

    x1 = x_ref[...]
    off = 0
    for o_ref, kk in zip(o_refs, splits):
        x1 = x1 + _dot(o_ref[...], wout_ref[off:off + kk, :])
        off += kk
    h2 = _rms(x1, gffn_ref[...]).astype(BF16)

    carry_mode = seq_len >= tm
    rows = min(seq_len, tm)
    nseq = tm // rows
    tiles_per_seq = max(seq_len // tm, 1)
    first = (i % tiles_per_seq) == 0

    for c in range(D_FF // FF_CHUNK):
        cs = slice(c * FF_CHUNK, (c + 1) * FF_CHUNK)
        gate = _dot(h2, wg_ref[:, cs])
        up = _dot(h2, wu_ref[:, cs])
        w0 = cw_ref[0:1, cs]
        w1 = cw_ref[1:2, cs]
        w2 = cw_ref[2:3, cs]
        bias = cb_ref[:, cs]
        for s in range(nseq):
            rs = slice(s * rows, (s + 1) * rows)
            g_s = gate[rs]
            if carry_mode:
                @pl.when(first)
                def _():
                    gate_ref[SUBLANES - 2:SUBLANES, :] = st_ref[0, :, cs]

                @pl.when(jnp.logical_not(first))
                def _():
                    gate_ref[0:SUBLANES, :] = carry_ref[:, cs]
            else:
                gate_ref[SUBLANES - 2:SUBLANES, :] = st_ref[s, :, cs]
            gate_ref[SUBLANES:SUBLANES + rows, :] = g_s
            conv_ref[s, :, cs] = g_s[rows - 2:rows]
            if carry_mode:
                carry_ref[:, cs] = g_s[rows - SUBLANES:rows]
            g2 = gate_ref[SUBLANES - 2:SUBLANES - 2 + rows, :]
            g1 = gate_ref[SUBLANES - 1:SUBLANES - 1 + rows, :]
            conv = bias + g2 * w0
            conv = conv + g1 * w1
            conv = conv + g_s * w2
            act = 0.5 * conv * (1.0 + lax.erf(conv * math.sqrt(0.5)))
            act_ref[rs, cs] = (act * up[rs]).astype(BF16)

    out = x1 + _dot(act_ref[...], wd_ref[...])
    if final_norm:
        out = _rms(out, gfin_ref[...])
    y_ref[...] = out


def _ffn(x, o_list, wout, gffn, wg, wu, cw, cb, wd, state, gfin, seq_len, final_norm):
    r = x.shape[0]
    tm = min(ROW_TILE, r)
    assert seq_len % tm == 0 or tm % seq_len == 0
    nseq = max(tm // seq_len, 1)
    tiles_per_seq = max(seq_len // tm, 1)
    splits = tuple(o.shape[1] for o in o_list)
    row = lambda n: pl.BlockSpec((tm, n), lambda i: (i, 0))
    st_spec = pl.BlockSpec((nseq, CONV_W - 1, D_FF), lambda i: (i // tiles_per_seq, 0, 0))
    kern = functools.partial(_ffn_kernel, n_o=len(o_list), splits=splits, tm=tm, seq_len=seq_len,
                             final_norm=final_norm)
    return pl.pallas_call(
        kern,
        grid=(r // tm,),
        in_specs=[row(D_MODEL)] + [row(k) for k in splits]
                 + [_const_spec((sum(splits), D_MODEL)), _const_spec((1, D_MODEL)),
                    _const_spec((D_MODEL, D_FF)), _const_spec((D_MODEL, D_FF)),
                    _const_spec((CONV_W, D_FF)), _const_spec((1, D_FF)), _const_spec((D_FF, D_MODEL)),
                    st_spec, _const_spec((1, D_MODEL))],
        out_specs=[row(D_MODEL), st_spec],
        out_shape=(jax.ShapeDtypeStruct((r, D_MODEL), F32), jax.ShapeDtypeStruct(state.shape, F32)),
        scratch_shapes=[pltpu.VMEM((SUBLANES + min(seq_len, tm), FF_CHUNK), F32),
                        pltpu.VMEM((SUBLANES, D_FF), F32),
                        pltpu.VMEM((tm, D_FF), BF16)],
        compiler_params=_cparams("arbitrary"),
        name="outproj_convglu",
    )(x, *o_list, wout, gffn, wg, wu, cw, cb, wd, state, gfin)


def _rope_tables(past, t, reps):
    half = ROPE_DIM // 2
    inv = ROPE_THETA ** (-jnp.arange(half, dtype=F32) / half)
    pos = past + jnp.arange(t, dtype=jnp.int32)
    ang = pos.astype(F32)[:, None] * inv[None, :]
    cos = jnp.tile(jnp.cos(ang), (reps, 2 * LANES // ROPE_DIM))
    sin = jnp.tile(jnp.sin(ang), (reps, 2 * LANES // ROPE_DIM))
    return cos, sin


def _rotate_half_cols(w):
    half = ROPE_DIM // 2
    return jnp.concatenate([-w[..., half:], w[..., :half]], axis=-1)


def _prep_ab_weights(w_in, b_f, w_q_up):
    q_lat, kv_lat, k_rope, qf, kf, vf, f_raw = jnp.split(
        w_in, [256, 384, 416, 928, 1440, 1952], axis=1)
    pad = jnp.zeros((D_MODEL, LANES - H_B), w_in.dtype)
    w1 = jnp.concatenate([q_lat, kv_lat, jnp.tile(k_rope, (1, 4)), jnp.tile(_rotate_half_cols(k_rope), (1, 4)),
                          f_raw, pad, qf, kf, vf], axis=1).astype(BF16)
    wq = w_q_up.reshape(Q_LORA, H_A, NOPE_DIM + ROPE_DIM)
    w_nope = wq[:, :, :NOPE_DIM].reshape(Q_LORA, H_A * NOPE_DIM)
    w_pe = wq[:, :, NOPE_DIM:]
    wq2 = jnp.concatenate([w_nope, w_pe.reshape(Q_LORA, H_A * ROPE_DIM),
                           _rotate_half_cols(w_pe).reshape(Q_LORA, H_A * ROPE_DIM)], axis=1).astype(BF16)
    bfp = jnp.concatenate([b_f, jnp.zeros((LANES - H_B,), b_f.dtype)]).reshape(1, LANES)
    return w1, wq2, bfp


def _pad_time(a, total):
    pad = total - a.shape[1]
    if pad == 0:
        return a
    return jnp.pad(a, ((0, 0), (0, pad)) + ((0, 0),) * (a.ndim - 2))


def _round_up(n, m):
    return (n + m - 1) // m * m


def _layer_ab(x, caches, p, batch, t, past):
    r = batch * t
    cos, sin = _rope_tables(past, t, batch)
    (ckv, kpe, kf, vf, logf, qcat, kcat, qfe, qfo, kfb, vfb) = _proj_ab(
        x, p["g_mix"], p["w1"], p["g_q"], p["g_kv"], p["bfp"], p["wq"], p["wkv"], cos, sin)

    tk = min(ATTN_TILE, past + t)
    tkp = _round_up(past + t, tk)
    rs = lambda a: a.reshape(batch, t, a.shape[-1])
    kcat3, kfb3, vfb3, logf3 = rs(kcat), rs(kfb), rs(vfb), rs(logf)
    if past:
        ckv_c, kpe_c, fk_c, fv_c, flf_c = caches
        kcat_c = jnp.concatenate([ckv_c, jnp.tile(kpe_c, (1, 1, 4))], axis=-1).astype(BF16)
        kcat3 = jnp.concatenate([kcat_c, kcat3], axis=1)
        kfb3 = jnp.concatenate([fk_c.reshape(batch, past, -1).astype(BF16), kfb3], axis=1)
        vfb3 = jnp.concatenate([fv_c.reshape(batch, past, -1).astype(BF16), vfb3], axis=1)
        logf3 = jnp.concatenate([flf_c, logf3], axis=1)
    kcat3, kfb3, vfb3 = _pad_time(kcat3, tkp), _pad_time(kfb3, tkp), _pad_time(vfb3, tkp)

    o_a = _flash_mla(rs(qcat), kcat3, p["wkv"], past)

    l_cum = LANES * SUBLANES
    while l_cum < tkp:
        l_cum *= 2
    cum_t = _cumsum_time(jnp.swapaxes(_pad_time(logf3, l_cum), 1, 2))
    cum_q = jnp.swapaxes(cum_t[:, :, past:past + t], 1, 2)
    o_b = _flash_fox(rs(qfe), rs(qfo), kfb3, vfb3, cum_q, cum_t, past)

    rows = (ckv.reshape(batch, t, KV_LORA), kpe.reshape(batch, t, ROPE_DIM),
            kf.reshape(batch, t, H_B, HD_B), vf.reshape(batch, t, H_B, HD_B), logf.reshape(batch, t, H_B))
    return [o_a.reshape(r, -1), o_b.reshape(r, -1)], rows


def _layer_c(x, caches, p, batch, t, past, lambda_init):
    r = batch * t
    k, v, q1, q2, kb, vb = _proj_c(x, p["g_mix"], p["w_in"])
    tk = min(ATTN_TILE, past + t)
    tkp = _round_up(past + t, tk)
    rs = lambda a: a.reshape(batch, t, a.shape[-1])
    kb3, vb3 = rs(kb), rs(vb)
    if past:
        k_c, v_c = caches
        kb3 = jnp.concatenate([k_c.reshape(batch, past, -1).astype(BF16), kb3], axis=1)
        vb3 = jnp.concatenate([v_c.reshape(batch, past, -1).astype(BF16), vb3], axis=1)
    kb3, vb3 = _pad_time(kb3, tkp), _pad_time(vb3, tkp)
    tq = min(ATTN_TILE, t)
    o = _flash_diff(p["far_bias"], rs(q1), rs(q2), kb3, vb3, p["bias_tiles"][(tq, tk)],
                    p["lq1"], p["lk1"], p["lq2"], p["lk2"], p["g_sub"], past, lambda_init)
    rows = (k.reshape(batch, t, H_C, 2 * HD_C), v.reshape(batch, t, H_C, 2 * HD_C))
    return [o.reshape(r, -1)], rows


def _diff_lambda_init(layer_idx):
    return 0.8 - 0.6 * math.exp(-0.3 * layer_idx)


def kernel(x_prompt, x_sample, cache_mla_ckv_0, cache_mla_kpe_0, cache_fox_k_0, cache_fox_v_0, cache_fox_logf_0, cache_diff_k_1, cache_diff_v_1, cache_mla_ckv_2, cache_mla_kpe_2, cache_fox_k_2, cache_fox_v_2, cache_fox_logf_2, cache_diff_k_3, cache_diff_v_3, state_ffn_conv, g_norm_mix, g_norm_ffn, g_norm_final, w_in_ab, b_fgate, g_q_lora, g_kv_lora, w_q_up, w_kv_up, w_out_ab, w_in_c, lambda_q1, lambda_k1, lambda_q2, lambda_k2, g_subln, w_out_c, rel_bias, w_ffn_in, conv_w, conv_b, w_ffn_down):
    bp, tp, _ = x_prompt.shape
    bs, ts, _ = x_sample.shape
    past = cache_mla_ckv_0.shape[1]
    sample_caches = [
        (cache_mla_ckv_0, cache_mla_kpe_0, cache_fox_k_0, cache_fox_v_0, cache_fox_logf_0),
        (cache_diff_k_1, cache_diff_v_1),
        (cache_mla_ckv_2, cache_mla_kpe_2, cache_fox_k_2, cache_fox_v_2, cache_fox_logf_2),
        (cache_diff_k_3, cache_diff_v_3),
    ]
    groups = [dict(batch=bp, t=tp, past=0), dict(batch=bs, t=ts, past=past)]

    bias_tiles = {}
    for gdef in groups:
        tq = min(ATTN_TILE, gdef["t"])
        tk = min(ATTN_TILE, gdef["past"] + gdef["t"])
        if (tq, tk) not in bias_tiles:
            bias_tiles[(tq, tk)] = _bias_tiles(rel_bias, tq, tk)
    far_bias = rel_bias[NUM_BUCKETS // 2 - 1]

    xs = [x_prompt.reshape(bp * tp, D_MODEL), x_sample.reshape(bs * ts, D_MODEL)]
    conv_prev = [jnp.zeros((DEPTH, bp, CONV_W - 1, D_FF), F32), state_ffn_conv]
    rows_out = [[], []]
    conv_out = [[], []]
    gfin = g_norm_final.reshape(1, D_MODEL)
    for l in range(DEPTH):
        i = l // 2
        if l % 2 == 0:
            w1, wq2, bfp = _prep_ab_weights(w_in_ab[i], b_fgate[i], w_q_up[i])
            p = dict(g_mix=g_norm_mix[l].reshape(1, -1), w1=w1, wq=wq2, bfp=bfp,
                     g_q=g_q_lora[i].reshape(1, -1), g_kv=g_kv_lora[i].reshape(1, -1),
                     wkv=w_kv_up[i].astype(BF16))
            wout = w_out_ab[i].astype(BF16)
        else:
            p = dict(g_mix=g_norm_mix[l].reshape(1, -1), w_in=w_in_c[i].astype(BF16),
                     lq1=lambda_q1[i].reshape(1, -1), lk1=lambda_k1[i].reshape(1, -1),
                     lq2=lambda_q2[i].reshape(1, -1), lk2=lambda_k2[i].reshape(1, -1),
                     g_sub=g_subln[i].reshape(1, -1), far_bias=far_bias, bias_tiles=bias_tiles)
            wout = w_out_c[i].astype(BF16)
        wg = w_ffn_in[l][:, :D_FF].astype(BF16)
        wu = w_ffn_in[l][:, D_FF:].astype(BF16)
        wd = w_ffn_down[l].astype(BF16)
        for gi, gdef in enumerate(groups):
            caches = sample_caches[l] if gi == 1 else None
            if l % 2 == 0:
                o_list, rows = _layer_ab(xs[gi], caches, p, **gdef)
            else:
                o_list, rows = _layer_c(xs[gi], caches, p, lambda_init=_diff_lambda_init(l), **gdef)
            xs[gi], conv_rows = _ffn(xs[gi], o_list, wout, g_norm_ffn[l].reshape(1, -1), wg, wu,
                                     conv_w[l], conv_b[l].reshape(1, -1), wd, conv_prev[gi][l], gfin,
                                     seq_len=gdef["t"], final_norm=(l == DEPTH - 1))
            rows_out[gi].append(rows)
            conv_out[gi].append(conv_rows)

    y_prompt = xs[0].reshape(bp, tp, D_MODEL)
    y_sample = xs[1].reshape(bs, ts, D_MODEL)
    flat = lambda rows: tuple(a for layer_rows in rows for a in layer_rows)
    return ((y_prompt, y_sample) + flat(rows_out[0]) + (jnp.stack(conv_out[0], axis=0),)
            + flat(rows_out[1]) + (jnp.stack(conv_out[1], axis=0),))
```
